```python
import math
import jax
import jax.numpy as jnp
from jax import lax
import numpy as np

D_MODEL = 4096
BATCH = 2
SEQ = 4096
DEPTH = 4

N_EVEN = (DEPTH + 1) // 2
N_ODD = DEPTH // 2
EPS = 1e-6
BLOCK = 128
HEAD_DIM = 128
A_HEADS = 16
A_KV_HEADS = 4
A_GROUP = A_HEADS // A_KV_HEADS
WINDOW = 128
ROPE_THETA = 500000.0
ROPE_DIM = HEAD_DIM // 4
S5_WIDTH = 1024
S5_GROUP = 16
S5_GROUPS = S5_WIDTH // S5_GROUP
S5_STATE = 64
C_HEADS = 16
LRU_WIDTH = 2048
LRU_BLOCKS = 16
LRU_BLOCK_DIM = LRU_WIDTH // LRU_BLOCKS
LRU_CONV = 4
LRU_C = 8.0
D_FF = 10240
FFN_CONV = 3
PLE_DIM = 256

A_Q = A_HEADS * HEAD_DIM
A_KV = A_KV_HEADS * HEAD_DIM
EVEN_IN = A_Q + 2 * A_KV + S5_WIDTH
EVEN_OUT = A_Q + S5_WIDTH
C_W = C_HEADS * HEAD_DIM
ODD_IN = 3 * C_W + 2 * LRU_WIDTH
ODD_OUT = C_W + LRU_WIDTH

kernel_name = 'hybrid_swa_s5_stickbreak_rglru_trunk'


def rmsnorm(x, g):
    xf = x.astype(jnp.float32)
    y = xf * lax.rsqrt(jnp.mean(xf * xf, axis=-1, keepdims=True) + EPS)
    return (y * g.astype(jnp.float32)).astype(x.dtype)


def causal_depthwise_conv(x, w, b):
    width, seq = w.shape[0], x.shape[1]
    xp = jnp.pad(x, ((0, 0), (width - 1, 0), (0, 0)))
    y = b
    for tap in range(width):
        y = y + xp[:, tap:tap + seq] * w[tap]
    return y


def partial_rope(x, positions):
    half = ROPE_DIM // 2
    inv = ROPE_THETA ** (-jnp.arange(half, dtype=jnp.float32) * (2.0 / ROPE_DIM))
    ang = positions.astype(jnp.float32)[:, :, None] * inv
    cos = jnp.cos(ang)[:, :, None, :]
    sin = jnp.sin(ang)[:, :, None, :]
    xr = x[..., :ROPE_DIM].astype(jnp.float32)
    x1, x2 = xr[..., :half], xr[..., half:]
    rot = jnp.concatenate([x1 * cos - x2 * sin, x2 * cos + x1 * sin], axis=-1).astype(x.dtype)
    return jnp.concatenate([rot, x[..., ROPE_DIM:]], axis=-1)


def sliding_window_attention(q, k, v, sinks):
    bsz, seq = q.shape[0], q.shape[1]
    nb = seq // BLOCK
    qb = q.reshape(bsz, nb, BLOCK, A_KV_HEADS, A_GROUP, HEAD_DIM)

    def banded(t):
        cur = t.reshape(bsz, nb, BLOCK, A_KV_HEADS, HEAD_DIM)
        prev = jnp.pad(cur, ((0, 0), (1, 0), (0, 0), (0, 0), (0, 0)))[:, :-1]
        return jnp.concatenate([prev, cur], axis=2)

    kb, vb = banded(k), banded(v)
    logits = jnp.einsum('bnqkgd,bnskd->bnkgqs', qb, kb,
                        preferred_element_type=jnp.float32) * (HEAD_DIM ** -0.5)
    qpos = jnp.arange(BLOCK)[:, None] + BLOCK
    kpos = jnp.arange(2 * BLOCK)[None, :]
    blk = jnp.arange(nb)[:, None, None]
    valid = (kpos <= qpos) & (qpos - kpos < WINDOW) & (blk * BLOCK - BLOCK + kpos >= 0)
    logits = jnp.where(valid[None, :, None, None, :, :], logits, -jnp.inf)
    sink = sinks.astype(jnp.float32).reshape(A_KV_HEADS, A_GROUP)[None, None, :, :, None, None]
    m = jnp.maximum(jnp.max(logits, axis=-1, keepdims=True), sink)
    w = jnp.exp(logits - m)
    probs = (w / (jnp.sum(w, axis=-1, keepdims=True) + jnp.exp(sink - m))).astype(v.dtype)
    out = jnp.einsum('bnkgqs,bnskd->bnqkgd', probs, vb)
    return out.reshape(bsz, seq, A_Q)


def s5_layer(u, lam_re, lam_im, log_dt, b_re, b_im, c_re, c_im, d_skip, glu_w, glu_b):
    bsz, seq = u.shape[0], u.shape[1]
    uf = u.astype(jnp.float32).reshape(bsz, seq, S5_GROUPS, S5_GROUP)
    dt = jnp.exp(log_dt.astype(jnp.float32))[:, None]
    lr = jnp.minimum(lam_re.astype(jnp.float32), -1e-4)
    li = lam_im.astype(jnp.float32)
    mag = jnp.exp(lr * dt)
    ab_re, ab_im = mag * jnp.cos(li * dt), mag * jnp.sin(li * dt)
    nr, ni = ab_re - 1.0, ab_im
    den = lr * lr + li * li
    f_re, f_im = (nr * lr + ni * li) / den, (ni * lr - nr * li) / den
    br, bi = b_re.astype(jnp.float32), b_im.astype(jnp.float32)
    bb_re = f_re[..., None] * br - f_im[..., None] * bi
    bb_im = f_re[..., None] * bi + f_im[..., None] * br
    xu_re = jnp.einsum('bsgp,gnp->bsgn', uf, bb_re)
    xu_im = jnp.einsum('bsgp,gnp->bsgn', uf, bb_im)
    a_re = jnp.broadcast_to(ab_re, xu_re.shape)
    a_im = jnp.broadcast_to(ab_im, xu_im.shape)

    def combine(left, right):
        ar1, ai1, hr1, hi1 = left
        ar2, ai2, hr2, hi2 = right
        return (ar2 * ar1 - ai2 * ai1, ar2 * ai1 + ai2 * ar1,
                ar2 * hr1 - ai2 * hi1 + hr2, ar2 * hi1 + ai2 * hr1 + hi2)

    _, _, h_re, h_im = lax.associative_scan(combine, (a_re, a_im, xu_re, xu_im), axis=1)
    y = (jnp.einsum('bsgn,gpn->bsgp', h_re, c_re.astype(jnp.float32))
         - jnp.einsum('bsgn,gpn->bsgp', h_im, c_im.astype(jnp.float32))
         + d_skip.astype(jnp.float32).reshape(S5_GROUPS, S5_GROUP) * uf)
    g = jax.nn.gelu(y.reshape(bsz, seq, S5_WIDTH))
    out = g * jax.nn.sigmoid(g @ glu_w.astype(jnp.float32) + glu_b.astype(jnp.float32))
    return out.astype(u.dtype)


def stick_breaking_attention(q, k, v):
    bsz, seq = q.shape[0], q.shape[1]
    scale = HEAD_DIM ** -0.5
    outs = []
    for i in range(seq // BLOCK):
        q0, L = i * BLOCK, (i + 1) * BLOCK
        z = jnp.einsum('bqhd,bshd->bhqs', q[:, q0:L], k[:, :L],
                       preferred_element_type=jnp.float32) * scale
        strict = jnp.arange(L)[None, :] < (q0 + jnp.arange(BLOCK))[:, None]
        log_beta = jax.nn.log_sigmoid(z)
        log_stay = jnp.where(strict, jax.nn.log_sigmoid(-z), 0.0)
        between = lax.cumsum(log_stay, axis=3, reverse=True) - log_stay
        w = jnp.where(strict, jnp.exp(log_beta + between), 0.0).astype(v.dtype)
        outs.append(jnp.einsum('bhqs,bshd->bqhd', w, v[:, :L]))
    return jnp.concatenate(outs, axis=1).reshape(bsz, seq, C_W)


def rglru_block(xr, xg, conv_w, conv_b, wa, ba, wx, bx, lam):
    bsz, seq = xr.shape[0], xr.shape[1]
    xc = causal_depthwise_conv(xr, conv_w, conv_b)
    xb = xc.reshape(bsz, seq, LRU_BLOCKS, LRU_BLOCK_DIM)
    r = jax.nn.sigmoid(jnp.einsum('bshi,hij->bshj', xb, wa) + ba).reshape(bsz, seq, LRU_WIDTH)
    ig = jax.nn.sigmoid(jnp.einsum('bshi,hij->bshj', xb, wx) + bx).reshape(bsz, seq, LRU_WIDTH)
    log_a = -LRU_C * r.astype(jnp.float32) * jax.nn.softplus(-lam.astype(jnp.float32))
    a = jnp.exp(log_a)
    drive = jnp.sqrt(-jnp.expm1(2.0 * log_a)) * (ig * xc).astype(jnp.float32)

    def combine(left, right):
        a1, h1 = left
        a2, h2 = right
        return a2 * a1, a2 * h1 + h2

    _, h = lax.associative_scan(combine, (a, drive), axis=1)
    return h.astype(xr.dtype) * jax.nn.gelu(xg)


def conv_ffn(n, w_up, conv_w, conv_b, w_down):
    h = causal_depthwise_conv(n @ w_up, conv_w, conv_b)
    g, u = jnp.split(h, 2, axis=-1)
    return (jax.nn.silu(g) * u) @ w_down


def _normal(key, shape, scale):
    return jax.random.normal(key, shape, jnp.float32) * scale


def setup_inputs(seed: int = 0) -> dict:
    key = jax.random.key(seed)
    ks = iter(jax.random.split(key, 48))
    f32 = jnp.float32
    x = _normal(next(ks), (BATCH, SEQ, D_MODEL), 1.0)
    p = _normal(next(ks), (DEPTH, BATCH, SEQ, PLE_DIM), 1.0)
    positions = (jax.random.randint(next(ks), (BATCH, 1), 0, 1024, jnp.int32)
                 + jnp.arange(SEQ, dtype=jnp.int32)[None, :])
    mix_norm = 1.0 + _normal(next(ks), (DEPTH, D_MODEL), 0.02)
    ffn_norm = 1.0 + _normal(next(ks), (DEPTH, D_MODEL), 0.02)
    ple_norm = 1.0 + _normal(next(ks), (DEPTH, D_MODEL), 0.02)
    final_norm = 1.0 + _normal(next(ks), (D_MODEL,), 0.02)
    ab_w_in = _normal(next(ks), (N_EVEN, D_MODEL, EVEN_IN), D_MODEL ** -0.5)
    ab_w_out = _normal(next(ks), (N_EVEN, EVEN_OUT, D_MODEL), EVEN_OUT ** -0.5)
    attn_sinks = _normal(next(ks), (N_EVEN, A_HEADS), 0.5)
    s5_lam_re = -0.5 + _normal(next(ks), (N_EVEN, S5_GROUPS, S5_STATE), 0.01)
    s5_lam_im = (math.pi * jnp.arange(S5_STATE, dtype=f32)
                 + _normal(next(ks), (N_EVEN, S5_GROUPS, S5_STATE), 0.01))
    s5_log_dt = jax.random.uniform(next(ks), (N_EVEN, S5_GROUPS), f32, math.log(1e-3), math.log(1e-1))
    s5_b_re = _normal(next(ks), (N_EVEN, S5_GROUPS, S5_STATE, S5_GROUP), (2 * S5_GROUP) ** -0.5)
    s5_b_im = _normal(next(ks), (N_EVEN, S5_GROUPS, S5_STATE, S5_GROUP), (2 * S5_GROUP) ** -0.5)
    s5_c_re = _normal(next(ks), (N_EVEN, S5_GROUPS, S5_GROUP, S5_STATE), S5_STATE ** -0.5)
    s5_c_im = _normal(next(ks), (N_EVEN, S5_GROUPS, S5_GROUP, S5_STATE), S5_STATE ** -0.5)
    s5_d = _normal(next(ks), (N_EVEN, S5_WIDTH), 1.0)
    s5_glu_w = _normal(next(ks), (N_EVEN, S5_WIDTH, S5_WIDTH), S5_WIDTH ** -0.5)
    s5_glu_b = _normal(next(ks), (N_EVEN, S5_WIDTH), 0.01)
    cd_w_in = _normal(next(ks), (N_ODD, D_MODEL, ODD_IN), D_MODEL ** -0.5)
    cd_w_out = _normal(next(ks), (N_ODD, ODD_OUT, D_MODEL), ODD_OUT ** -0.5)
    lru_conv_w = _normal(next(ks), (N_ODD, LRU_CONV, LRU_WIDTH), LRU_CONV ** -0.5)
    lru_conv_b = _normal(next(ks), (N_ODD, LRU_WIDTH), 0.01)
    lru_wa = _normal(next(ks), (N_ODD, LRU_BLOCKS, LRU_BLOCK_DIM, LRU_BLOCK_DIM), LRU_BLOCK_DIM ** -0.5)
    lru_ba = _normal(next(ks), (N_ODD, LRU_BLOCKS, LRU_BLOCK_DIM), 0.01)
    lru_wx = _normal(next(ks), (N_ODD, LRU_BLOCKS, LRU_BLOCK_DIM, LRU_BLOCK_DIM), LRU_BLOCK_DIM ** -0.5)
    lru_bx = _normal(next(ks), (N_ODD, LRU_BLOCKS, LRU_BLOCK_DIM), 0.01)
    a_pow_c = jax.random.uniform(next(ks), (N_ODD, LRU_WIDTH), f32, 0.9, 0.999)
    s = a_pow_c ** (1.0 / LRU_C)
    lru_lambda = jnp.log(s) - jnp.log1p(-s)
    ffn_w_up = _normal(next(ks), (DEPTH, D_MODEL, 2 * D_FF), D_MODEL ** -0.5)
    ffn_conv_w = _normal(next(ks), (DEPTH, FFN_CONV, 2 * D_FF), FFN_CONV ** -0.5)
    ffn_conv_b = _normal(next(ks), (DEPTH, 2 * D_FF), 0.01)
    ffn_w_down = _normal(next(ks), (DEPTH, D_FF, D_MODEL), D_FF ** -0.5)
    ple_w_gate = _normal(next(ks), (DEPTH, D_MODEL, D_MODEL), D_MODEL ** -0.5)
    ple_w_proj = _normal(next(ks), (DEPTH, PLE_DIM, D_MODEL), PLE_DIM ** -0.5)
    return {'x': x, 'p': p, 'positions': positions,
            'mix_norm': mix_norm, 'ffn_norm': ffn_norm, 'ple_norm': ple_norm, 'final_norm': final_norm,
            'ab_w_in': ab_w_in, 'ab_w_out': ab_w_out, 'attn_sinks': attn_sinks,
            's5_lam_re': s5_lam_re, 's5_lam_im': s5_lam_im, 's5_log_dt': s5_log_dt,
            's5_b_re': s5_b_re, 's5_b_im': s5_b_im, 's5_c_re': s5_c_re, 's5_c_im': s5_c_im,
            's5_d': s5_d, 's5_glu_w': s5_glu_w, 's5_glu_b': s5_glu_b,
            'cd_w_in': cd_w_in, 'cd_w_out': cd_w_out, 'lru_conv_w': lru_conv_w, 'lru_conv_b': lru_conv_b,
            'lru_wa': lru_wa, 'lru_ba': lru_ba, 'lru_wx': lru_wx, 'lru_bx': lru_bx, 'lru_lambda': lru_lambda,
            'ffn_w_up': ffn_w_up, 'ffn_conv_w': ffn_conv_w, 'ffn_conv_b': ffn_conv_b, 'ffn_w_down': ffn_w_down,
            'ple_w_gate': ple_w_gate, 'ple_w_proj': ple_w_proj}


def reference(x, p, positions, mix_norm, ffn_norm, ple_norm, final_norm,
              ab_w_in, ab_w_out, attn_sinks,
              s5_lam_re, s5_lam_im, s5_log_dt, s5_b_re, s5_b_im, s5_c_re, s5_c_im,
              s5_d, s5_glu_w, s5_glu_b,
              cd_w_in, cd_w_out, lru_conv_w, lru_conv_b, lru_wa, lru_ba, lru_wx, lru_bx, lru_lambda,
              ffn_w_up, ffn_conv_w, ffn_conv_b, ffn_w_down, ple_w_gate, ple_w_proj):
    bsz, seq = x.shape[0], x.shape[1]
    h = x
    for i in range(DEPTH):
        j = i // 2
        n = rmsnorm(h, mix_norm[i])
        if i % 2 == 0:
            q, k, v, u = jnp.split(n @ ab_w_in[j], [A_Q, A_Q + A_KV, A_Q + 2 * A_KV], axis=-1)
            q = partial_rope(q.reshape(bsz, seq, A_HEADS, HEAD_DIM), positions)
            k = partial_rope(k.reshape(bsz, seq, A_KV_HEADS, HEAD_DIM), positions)
            v = v.reshape(bsz, seq, A_KV_HEADS, HEAD_DIM)
            ya = sliding_window_attention(q, k, v, attn_sinks[j])
            yb = s5_layer(u, s5_lam_re[j], s5_lam_im[j], s5_log_dt[j], s5_b_re[j], s5_b_im[j],
                          s5_c_re[j], s5_c_im[j], s5_d[j], s5_glu_w[j], s5_glu_b[j])
            h = h + jnp.concatenate([ya, yb], axis=-1) @ ab_w_out[j]
        else:
            q, k, v, xr, xg = jnp.split(n @ cd_w_in[j],
                                        [C_W, 2 * C_W, 3 * C_W, 3 * C_W + LRU_WIDTH], axis=-1)
            yc = stick_breaking_attention(q.reshape(bsz, seq, C_HEADS, HEAD_DIM),
                                          k.reshape(bsz, seq, C_HEADS, HEAD_DIM),
                                          v.reshape(bsz, seq, C_HEADS, HEAD_DIM))
            yd = rglru_block(xr, xg, lru_conv_w[j], lru_conv_b[j], lru_wa[j], lru_ba[j],
                             lru_wx[j], lru_bx[j], lru_lambda[j])
            h = h + jnp.concatenate([yc, yd], axis=-1) @ cd_w_out[j]
        h = h + conv_ffn(rmsnorm(h, ffn_norm[i]), ffn_w_up[i], ffn_conv_w[i], ffn_conv_b[i], ffn_w_down[i])
        gate = jax.nn.sigmoid(rmsnorm(h, ple_norm[i]) @ ple_w_gate[i])
        h = h + gate * (p[i] @ ple_w_proj[i])
    return rmsnorm(h, final_norm)
```

```python
import functools
import math

import jax
import jax.numpy as jnp
from jax import lax
from jax.experimental import pallas as pl
from jax.experimental.pallas import tpu as pltpu

F32 = jnp.float32
BF16 = jnp.bfloat16

EPS = 1e-6
HEAD_DIM = 128
SWA_HEADS = 16
SWA_KV_HEADS = 4
SWA_GROUP = SWA_HEADS // SWA_KV_HEADS
SWA_BLOCK = 128
ROPE_THETA = 500000.0
ROPE_DIM = HEAD_DIM // 4
S5_GROUP = 16
S5_STATE = 64
S5_CHUNK = 16
SB_HEADS = 16
LRU_BLOCKS = 16
LRU_C = 8.0
LANES = 128

VMEM_LIMIT_BYTES = 56 * 1024 * 1024
MXU_COLS = 256


def _params(*semantics):
    return pltpu.CompilerParams(dimension_semantics=semantics,
                                vmem_limit_bytes=VMEM_LIMIT_BYTES)


def _dot(a, b):
    return jnp.dot(a, b, preferred_element_type=F32)


def _dot_nt(a, b):
    return lax.dot_general(a, b, (((1,), (1,)), ((), ())), preferred_element_type=F32)


def _gelu_tanh(x):
    c = math.sqrt(2.0 / math.pi)
    return 0.5 * x * (1.0 + jnp.tanh(c * (x + 0.044715 * (x * x * x))))


def _shift_rows(x, prev8, shift):
    main = pltpu.roll(x, shift, axis=0)
    head = pltpu.roll(jnp.concatenate([prev8, x[:8]], axis=0), shift, axis=0)[8:]
    return jnp.concatenate([head, main[8:]], axis=0)


def _rmsnorm_kernel(x_ref, g_ref, o_ref):
    x = x_ref[...]
    ms = jnp.mean(x * x, axis=-1, keepdims=True)
    o_ref[...] = (x * lax.rsqrt(ms + EPS) * g_ref[...]).astype(o_ref.dtype)


def _rmsnorm(x, g, out_dtype):
    m, d = x.shape
    tm = min(256, m)
    return pl.pallas_call(
        _rmsnorm_kernel,
        grid=(m // tm,),
        in_specs=[pl.BlockSpec((tm, d), lambda i: (i, 0)),
                  pl.BlockSpec((1, d), lambda i: (0, 0))],
        out_specs=pl.BlockSpec((tm, d), lambda i: (i, 0)),
        out_shape=jax.ShapeDtypeStruct((m, d), out_dtype),
        compiler_params=_params("parallel"),
        name="rmsnorm",
    )(x, g.reshape(1, d).astype(F32))


def _mm_kernel(*refs, nk, has_res):
    if has_res:
        x_ref, w_ref, r_ref, o_ref, *scratch = refs
    else:
        x_ref, w_ref, o_ref, *scratch = refs
        r_ref = None
    tn = o_ref.shape[1]
    if nk == 1:
        for c in range(tn // MXU_COLS):
            sl = slice(c * MXU_COLS, (c + 1) * MXU_COLS)
            acc = _dot(x_ref[...], w_ref[:, sl])
            if has_res:
                acc = acc + r_ref[:, sl]
            o_ref[:, sl] = acc.astype(o_ref.dtype)
        return
    acc_ref = scratch[0]
    k = pl.program_id(2)
    part = _dot(x_ref[...], w_ref[...])

    @pl.when(k == 0)
    def _():
        acc_ref[...] = part + r_ref[...] if has_res else part

    @pl.when(k > 0)
    def _():
        acc_ref[...] += part

    @pl.when(k == nk - 1)
    def _():
        o_ref[...] = acc_ref[...].astype(o_ref.dtype)


def _matmul(x, w, out_dtype, residual=None, tn=1024, tk=None):
    m, kdim = x.shape
    n = w.shape[1]
    tm = min(1024, m)
    tn = min(tn, n)
    tk = kdim if tk is None else tk
    nk = kdim // tk
    has_res = residual is not None
    in_specs = [pl.BlockSpec((tm, tk), lambda i, j, k: (i, k)),
                pl.BlockSpec((tk, tn), lambda i, j, k: (k, j))]
    args = [x, w]
    if has_res:
        in_specs.append(pl.BlockSpec((tm, tn), lambda i, j, k: (i, j)))
        args.append(residual)
    scratch = [pltpu.VMEM((tm, tn), F32)] if nk > 1 else []
    return pl.pallas_call(
        functools.partial(_mm_kernel, nk=nk, has_res=has_res),
        grid=(m // tm, n // tn, nk),
        in_specs=in_specs,
        out_specs=pl.BlockSpec((tm, tn), lambda i, j, k: (i, j)),
        out_shape=jax.ShapeDtypeStruct((m, n), out_dtype),
        scratch_shapes=scratch,
        compiler_params=_params("parallel", "parallel", "arbitrary"),
        name="proj_res" if has_res else "proj",
    )(*args)


def _ffn_up_kernel(x_ref, wg_ref, wu_ref, cwg_ref, cwu_ref, cbg_ref, cbu_ref, o_ref,
                   carry_g, carry_u, *, tiles_per_seq):
    i = pl.program_id(0)
    j = pl.program_id(1)
    seq_start = (i % tiles_per_seq) == 0
    tm, tn = o_ref.shape

    def conv(acc, carry_ref, cw_ref, cb_ref, sl):
        prev8 = jnp.where(seq_start, 0.0, carry_ref[j, :, sl])
        carry_ref[j, :, sl] = acc[tm - 8:, :]
        x1 = _shift_rows(acc, prev8, 1)
        x2 = _shift_rows(acc, prev8, 2)
        return (cb_ref[:, sl] + cw_ref[0:1, sl] * x2 + cw_ref[1:2, sl] * x1
                + cw_ref[2:3, sl] * acc)

    for c in range(tn // MXU_COLS):
        sl = slice(c * MXU_COLS, (c + 1) * MXU_COLS)
        g = conv(_dot(x_ref[...], wg_ref[:, sl]), carry_g, cwg_ref, cbg_ref, sl)
        u = conv(_dot(x_ref[...], wu_ref[:, sl]), carry_u, cwu_ref, cbu_ref, sl)
        o_ref[:, sl] = (g * jax.nn.sigmoid(g) * u).astype(o_ref.dtype)


def _ffn_up(x, w_up, conv_w, conv_b, seq):
    m, d = x.shape
    d_ff = w_up.shape[1] // 2
    tm = min(1024, seq)
    tn = 512
    nj = d_ff // tn
    conv_b = conv_b.reshape(1, 2 * d_ff)
    return pl.pallas_call(
        functools.partial(_ffn_up_kernel, tiles_per_seq=seq // tm),
        grid=(m // tm, nj),
        in_specs=[pl.BlockSpec((tm, d), lambda i, j: (i, 0)),
                  pl.BlockSpec((d, tn), lambda i, j: (0, j)),
                  pl.BlockSpec((d, tn), lambda i, j: (0, j + nj)),
                  pl.BlockSpec((3, tn), lambda i, j: (0, j)),
                  pl.BlockSpec((3, tn), lambda i, j: (0, j + nj)),
                  pl.BlockSpec((1, tn), lambda i, j: (0, j)),
                  pl.BlockSpec((1, tn), lambda i, j: (0, j + nj))],
        out_specs=pl.BlockSpec((tm, tn), lambda i, j: (i, j)),
        out_shape=jax.ShapeDtypeStruct((m, d_ff), BF16),
        scratch_shapes=[pltpu.VMEM((nj, 8, tn), F32), pltpu.VMEM((nj, 8, tn), F32)],
        compiler_params=_params("arbitrary", "arbitrary"),
        name="ffn_up_conv_gate",
    )(x, w_up, w_up, conv_w, conv_w, conv_b, conv_b)


def _ple_kernel(n_ref, wg_ref, p_ref, wp_ref, h_ref, o_ref):
    tn = o_ref.shape[1]
    for c in range(tn // MXU_COLS):
        sl = slice(c * MXU_COLS, (c + 1) * MXU_COLS)
        gate = jax.nn.sigmoid(_dot(n_ref[...], wg_ref[:, sl]))
        proj = _dot(p_ref[...], wp_ref[:, sl])
        o_ref[:, sl] = h_ref[:, sl] + gate * proj


def _ple(n, wg, p, wp, h):
    m, d = n.shape
    pd = p.shape[1]
    tm = min(1024, m)
    tn = 512
    return pl.pallas_call(
        _ple_kernel,
        grid=(m // tm, d // tn),
        in_specs=[pl.BlockSpec((tm, d), lambda i, j: (i, 0)),
                  pl.BlockSpec((d, tn), lambda i, j: (0, j)),
                  pl.BlockSpec((tm, pd), lambda i, j: (i, 0)),
                  pl.BlockSpec((pd, tn), lambda i, j: (0, j)),
                  pl.BlockSpec((tm, tn), lambda i, j: (i, j))],
        out_specs=pl.BlockSpec((tm, tn), lambda i, j: (i, j)),
        out_shape=jax.ShapeDtypeStruct((m, d), F32),
        compiler_params=_params("parallel", "parallel"),
        name="ple_gate",
    )(n, wg, p, wp, h)


def _swa_kernel(sink_ref, inv_ref, q_ref, kc_ref, kp_ref, vc_ref, vp_ref, pc_ref, pp_ref,
                o_ref):
    n = pl.program_id(1)
    blk = SWA_BLOCK
    half = ROPE_DIM // 2
    lane = lax.broadcasted_iota(jnp.int32, (1, HEAD_DIM), 1)

    def tables(pos_ref):
        ang = pos_ref[...].astype(F32) * inv_ref[...]
        cos, sin = jnp.cos(ang), jnp.sin(ang)
        from_below = jnp.where((lane >= half) & (lane < 2 * half), sin, 0.0)
        from_above = jnp.where(lane < half, -sin, 0.0)
        return cos, from_below, from_above

    def rope(x, tab):
        cos, from_below, from_above = tab
        x = x.astype(F32)
        return (x * cos + pltpu.roll(x, half, axis=1) * from_below
                + pltpu.roll(x, HEAD_DIM - half, axis=1) * from_above).astype(BF16)

    tab_c = tables(pc_ref)
    tab_p = tables(pp_ref)
    row = lax.broadcasted_iota(jnp.int32, (SWA_GROUP * blk, 2 * blk), 0) % blk
    col = lax.broadcasted_iota(jnp.int32, (SWA_GROUP * blk, 2 * blk), 1)
    valid = (col > row) & (col <= row + blk) & ((col >= blk) | (n > 0))
    scale = HEAD_DIM ** -0.5

    for kh in range(SWA_KV_HEADS):
        ksl = slice(kh * HEAD_DIM, (kh + 1) * HEAD_DIM)
        k2 = jnp.concatenate([rope(kp_ref[:, ksl], tab_p), rope(kc_ref[:, ksl], tab_c)], axis=0)
        v2 = jnp.concatenate([vp_ref[:, ksl], vc_ref[:, ksl]], axis=0)
        heads = [kh * SWA_GROUP + g for g in range(SWA_GROUP)]
        qs = jnp.concatenate(
            [rope(q_ref[:, h * HEAD_DIM:(h + 1) * HEAD_DIM], tab_c) for h in heads], axis=0)
        sink = jnp.concatenate(
            [jnp.full((blk, 1), sink_ref[h], F32) for h in heads], axis=0)
        logits = jnp.where(valid, _dot_nt(qs, k2) * scale, -jnp.inf)
        mx = jnp.maximum(jnp.max(logits, axis=-1, keepdims=True), sink)
        w = jnp.exp(logits - mx)
        den = jnp.sum(w, axis=-1, keepdims=True) + jnp.exp(sink - mx)
        out = _dot((w / den).astype(BF16), v2)
        for g, h in enumerate(heads):
            o_ref[:, h * HEAD_DIM:(h + 1) * HEAD_DIM] = out[g * blk:(g + 1) * blk].astype(o_ref.dtype)


def _swa(qkv, positions, sinks, batch, seq):
    blk = SWA_BLOCK
    nb = seq // blk
    qw = SWA_HEADS * HEAD_DIM
    kw = SWA_KV_HEADS * HEAD_DIM
    kcol = qw // kw
    half = ROPE_DIM // 2
    inv = ROPE_THETA ** (-jnp.arange(half, dtype=F32) * (2.0 / ROPE_DIM))
    inv = jnp.concatenate([inv, inv, jnp.zeros((HEAD_DIM - 2 * half,), F32)]).reshape(1, HEAD_DIM)
    pos = positions.reshape(batch * seq, 1)

    def cur(b, n):
        return b * nb + n

    def prev(b, n):
        return b * nb + jnp.maximum(n - 1, 0)

    return pl.pallas_call(
        _swa_kernel,
        grid=(batch, nb),
        in_specs=[pl.BlockSpec(memory_space=pltpu.SMEM),
                  pl.BlockSpec((1, HEAD_DIM), lambda b, n: (0, 0)),
                  pl.BlockSpec((blk, qw), lambda b, n: (cur(b, n), 0)),
                  pl.BlockSpec((blk, kw), lambda b, n: (cur(b, n), kcol)),
                  pl.BlockSpec((blk, kw), lambda b, n: (prev(b, n), kcol)),
                  pl.BlockSpec((blk, kw), lambda b, n: (cur(b, n), kcol + 1)),
                  pl.BlockSpec((blk, kw), lambda b, n: (prev(b, n), kcol + 1)),
                  pl.BlockSpec((blk, 1), lambda b, n: (cur(b, n), 0)),
                  pl.BlockSpec((blk, 1), lambda b, n: (prev(b, n), 0))],
        out_specs=pl.BlockSpec((blk, qw), lambda b, n: (cur(b, n), 0)),
        out_shape=jax.ShapeDtypeStruct((batch * seq, qw), BF16),
        compiler_params=_params("parallel", "parallel"),
        name="swa_attention",
    )(sinks.astype(F32), inv, qkv, qkv, qkv, qkv, qkv, pos, pos)


def _s5_operators(lam_re, lam_im, log_dt, b_re, b_im, c_re, c_im, d_skip, n_steps):
    hi = lax.Precision.HIGHEST
    t = S5_CHUNK
    g, n = lam_re.shape
    p = b_re.shape[-1]
    dt = jnp.exp(log_dt.astype(F32))[:, None]
    lr = jnp.minimum(lam_re.astype(F32), -1e-4)
    li = lam_im.astype(F32)
    mag = jnp.exp(lr * dt)
    ab_re, ab_im = mag * jnp.cos(li * dt), mag * jnp.sin(li * dt)
    nr, ni = ab_re - 1.0, ab_im
    den = lr * lr + li * li
    f_re, f_im = (nr * lr + ni * li) / den, (ni * lr - nr * li) / den
    br, bi = b_re.astype(F32), b_im.astype(F32)
    bb_re = f_re[..., None] * br - f_im[..., None] * bi
    bb_im = f_re[..., None] * bi + f_im[..., None] * br

    def power(k):
        k = k.astype(F32)[None, :, None]
        m = jnp.exp(k * (lr * dt)[:, None, :])
        ph = k * (li * dt)[:, None, :]
        return m * jnp.cos(ph), m * jnp.sin(ph)

    pw_re, pw_im = power(jnp.arange(t + 1))
    cr, ci = c_re.astype(F32), c_im.astype(F32)
    ca_re = cr[:, None] * pw_re[:, :, None, :] - ci[:, None] * pw_im[:, :, None, :]
    ca_im = cr[:, None] * pw_im[:, :, None, :] + ci[:, None] * pw_re[:, :, None, :]
    kern = (jnp.einsum('gkpn,gnq->gkpq', ca_re[:, :t], bb_re, precision=hi)
            - jnp.einsum('gkpn,gnq->gkpq', ca_im[:, :t], bb_im, precision=hi))
    jj = jnp.arange(t)[:, None]
    ii = jnp.arange(t)[None, :]
    lag = ii - jj
    toe = jnp.where((lag >= 0)[None, :, :, None, None], kern[:, jnp.maximum(lag, 0)], 0.0)
    local_op = toe.transpose(0, 1, 4, 2, 3).reshape(g, t * p, t * p)

    rev_re, rev_im = pw_re[:, t - 1::-1][:, :t], pw_im[:, t - 1::-1][:, :t]
    st_re = rev_re[..., None] * bb_re[:, None] - rev_im[..., None] * bb_im[:, None]
    st_im = rev_re[..., None] * bb_im[:, None] + rev_im[..., None] * bb_re[:, None]
    state_op = jnp.concatenate([st_re.transpose(0, 1, 3, 2), st_im.transpose(0, 1, 3, 2)],
                               axis=-1).reshape(g, t * p, 2 * n)

    carry_op = jnp.concatenate([ca_re[:, 1:].transpose(0, 3, 1, 2),
                                -ca_im[:, 1:].transpose(0, 3, 1, 2)], axis=1)
    carry_op = carry_op.reshape(g, 2 * n, t * p)

    ar, ai = power(t * (2 ** jnp.arange(n_steps)))
    step_mul = jnp.stack([jnp.concatenate([ar, ar], axis=-1),
                          jnp.concatenate([-ai, ai], axis=-1)], axis=2)
    step_mul = step_mul.reshape(g, 2 * n_steps, 2 * n)
    d_rep = jnp.tile(d_skip.astype(F32).reshape(g, 1, p), (1, t, 1)).reshape(g, 1, t * p)
    return (local_op.astype(BF16), state_op.astype(BF16), carry_op.astype(BF16), step_mul, d_rep)


def _s5_kernel(u_ref, local_ref, state_ref, carry_ref, mul_ref, d_ref, o_ref, *,
               chunks_per_seq, n_steps):
    u = u_ref[0]
    ub = u.astype(BF16)
    y = _dot(ub, local_ref[0])
    h = _dot(ub, state_ref[0])
    nstate = h.shape[1] // 2
    row = lax.broadcasted_iota(jnp.int32, h.shape, 0) % chunks_per_seq
    for s in range(n_steps):
        sh = 1 << s
        hs = jnp.where(row >= sh, pltpu.roll(h, sh, axis=0), 0.0)
        h = (h + mul_ref[0, 2 * s:2 * s + 1, :] * hs
             + mul_ref[0, 2 * s + 1:2 * s + 2, :] * pltpu.roll(hs, nstate, axis=1))
    h_prev = jnp.where(row >= 1, pltpu.roll(h, 1, axis=0), 0.0)
    y = y + _dot(h_prev.astype(BF16), carry_ref[0]) + d_ref[0] * u
    o_ref[0] = _gelu_tanh(y)


def _s5(u, ops, batch, seq):
    local_op, state_op, carry_op, step_mul, d_rep = ops
    g = local_op.shape[0]
    t, p = S5_CHUNK, S5_GROUP
    nc = batch * seq // t
    tp = t * p
    n2 = state_op.shape[2]
    n_steps = step_mul.shape[1] // 2
    ut = u.reshape(nc, t, g, p).transpose(2, 0, 1, 3).reshape(g, nc, tp)
    out = pl.pallas_call(
        functools.partial(_s5_kernel, chunks_per_seq=seq // t, n_steps=n_steps),
        grid=(g,),
        in_specs=[pl.BlockSpec((1, nc, tp), lambda i: (i, 0, 0)),
                  pl.BlockSpec((1, tp, tp), lambda i: (i, 0, 0)),
                  pl.BlockSpec((1, tp, n2), lambda i: (i, 0, 0)),
                  pl.BlockSpec((1, n2, tp), lambda i: (i, 0, 0)),
                  pl.BlockSpec((1, 2 * n_steps, n2), lambda i: (i, 0, 0)),
                  pl.BlockSpec((1, 1, tp), lambda i: (i, 0, 0))],
        out_specs=pl.BlockSpec((1, nc, tp), lambda i: (i, 0, 0)),
        out_shape=jax.ShapeDtypeStruct((g, nc, tp), F32),
        compiler_params=_params("parallel"),
        name="s5_chunked",
    )(ut, local_op, state_op, carry_op, step_mul, d_rep)
    return out.reshape(g, nc, t, p).transpose(1, 2, 0, 3).reshape(batch * seq, g * p)


def _glu_kernel(g_ref, w_ref, b_ref, o_ref):
    g = g_ref[...]
    z = _dot(g.astype(BF16), w_ref[...]) + b_ref[...]
    o_ref[...] = (g * jax.nn.sigmoid(z)).astype(o_ref.dtype)


def _glu(g, w, b):
    m, d = g.shape
    tm = min(512, m)
    return pl.pallas_call(
        _glu_kernel,
        grid=(m // tm,),
        in_specs=[pl.BlockSpec((tm, d), lambda i: (i, 0)),
                  pl.BlockSpec((d, d), lambda i: (0, 0)),
                  pl.BlockSpec((1, d), lambda i: (0, 0))],
        out_specs=pl.BlockSpec((tm, d), lambda i: (i, 0)),
        out_shape=jax.ShapeDtypeStruct((m, d), BF16),
        compiler_params=_params("parallel"),
        name="s5_glu",
    )(g, w, b.reshape(1, d).astype(F32))


def _stickbreak_kernel(q_ref, k_ref, v_ref, o_ref, *, tq):
    qi = pl.program_id(2)
    blk = LANES
    scale = HEAD_DIM ** -0.5
    q = q_ref[...]
    suffix = (lax.broadcasted_iota(jnp.int32, (blk, blk), 0)
              > lax.broadcasted_iota(jnp.int32, (blk, blk), 1)).astype(BF16)

    def block(start, acc, later, strict):
        k = k_ref[pl.ds(start, blk), :]
        v = v_ref[pl.ds(start, blk), :]
        z = _dot_nt(q, k) * scale
        softplus = jnp.log1p(jnp.exp(-jnp.abs(z)))
        log_beta = jnp.minimum(z, 0.0) - softplus
        log_stay = log_beta - z
        if strict is not None:
            log_stay = jnp.where(strict, log_stay, 0.0)
        hi = log_stay.astype(BF16)
        lo = (log_stay - hi.astype(F32)).astype(BF16)
        between = _dot(hi, suffix) + _dot(lo, suffix) + later
        w = jnp.exp(log_beta + between)
        if strict is not None:
            w = jnp.where(strict, w, 0.0)
        acc = acc + _dot(w.astype(BF16), v)
        later = later + jnp.sum(log_stay, axis=-1, keepdims=True)
        return acc, later

    acc = jnp.zeros((tq, HEAD_DIM), F32)
    later = jnp.zeros((tq, 1), F32)
    q0 = qi * tq
    row = lax.broadcasted_iota(jnp.int32, (tq, blk), 0)
    col = lax.broadcasted_iota(jnp.int32, (tq, blk), 1)
    for d in reversed(range(tq // blk)):
        acc, later = block(pl.multiple_of(q0 + d * blk, blk), acc, later, col + d * blk < row)

    def body(it, carry):
        start = pl.multiple_of(q0 - (it + 1) * blk, blk)
        return block(start, carry[0], carry[1], None)

    acc, later = lax.fori_loop(0, q0 // blk, body, (acc, later))
    o_ref[...] = acc.astype(o_ref.dtype)


def _stickbreak(qkv, batch, seq):
    tq = min(256, seq)
    nq = seq // tq
    hd = HEAD_DIM
    return pl.pallas_call(
        functools.partial(_stickbreak_kernel, tq=tq),
        grid=(batch, SB_HEADS, nq),
        in_specs=[pl.BlockSpec((tq, hd), lambda b, h, i: (b * nq + i, h)),
                  pl.BlockSpec((seq, hd), lambda b, h, i: (b, SB_HEADS + h)),
                  pl.BlockSpec((seq, hd), lambda b, h, i: (b, 2 * SB_HEADS + h))],
        out_specs=pl.BlockSpec((tq, hd), lambda b, h, i: (b * nq + i, h)),
        out_shape=jax.ShapeDtypeStruct((batch * seq, SB_HEADS * hd), BF16),
        compiler_params=_params("parallel", "parallel", "arbitrary"),
        name="stickbreak_attention",
    )(qkv, qkv, qkv)


def _lru_kernel(xr_ref, xg_ref, cw_ref, cb_ref, wax_ref, ba_ref, bx_ref, lam_ref, o_ref,
                x_tail, h_last, *, n_heads):
    t = pl.program_id(2)
    tt, width = o_ref.shape
    hd = width // n_heads
    x = xr_ref[...]
    prev8 = jnp.where(t == 0, 0.0, x_tail[...])
    x_tail[...] = x[tt - 8:, :]
    xc = (cb_ref[...] + cw_ref[0:1, :] * _shift_rows(x, prev8, 3)
          + cw_ref[1:2, :] * _shift_rows(x, prev8, 2)
          + cw_ref[2:3, :] * _shift_rows(x, prev8, 1) + cw_ref[3:4, :] * x)
    gates = [_dot(xc[:, h * hd:(h + 1) * hd].astype(BF16), wax_ref[h]) for h in range(n_heads)]
    r = jax.nn.sigmoid(jnp.concatenate([gz[:, :hd] for gz in gates], axis=1) + ba_ref[...])
    ig = jax.nn.sigmoid(jnp.concatenate([gz[:, hd:] for gz in gates], axis=1) + bx_ref[...])
    lam = lam_ref[...]
    softplus_neg_lam = jnp.maximum(-lam, 0.0) + jnp.log1p(jnp.exp(-jnp.abs(lam)))
    log_a = -LRU_C * r * softplus_neg_lam
    a = jnp.exp(log_a)
    drive = jnp.sqrt(-jnp.tanh(log_a) * (1.0 + a * a)) * (ig * xc)
    row = lax.broadcasted_iota(jnp.int32, (tt, width), 0)
    h = drive
    s = 1
    while s < tt:
        keep = row >= s
        h = a * jnp.where(keep, pltpu.roll(h, s, axis=0), 0.0) + h
        a = a * jnp.where(keep, pltpu.roll(a, s, axis=0), 1.0)
        s *= 2
    h0 = jnp.where(t == 0, 0.0, h_last[...])
    h = h + a * h0
    h_last[...] = h[tt - 1:, :]
    o_ref[...] = (h * _gelu_tanh(xg_ref[...])).astype(o_ref.dtype)


def _lru(xrg, conv_w, conv_b, wa, ba, wx, bx, lam, batch, seq):
    width = xrg.shape[1] // 2
    hd = width // LRU_BLOCKS
    n_heads = 4
    tw = n_heads * hd
    ncol = width // tw
    tt = min(256, seq)
    nt = seq // tt
    wax = jnp.concatenate([wa, wx], axis=-1).astype(BF16)
    row = lambda v: v.reshape(1, width).astype(F32)
    return pl.pallas_call(
        functools.partial(_lru_kernel, n_heads=n_heads),
        grid=(batch, ncol, nt),
        in_specs=[pl.BlockSpec((tt, tw), lambda b, c, t: (b * nt + t, c)),
                  pl.BlockSpec((tt, tw), lambda b, c, t: (b * nt + t, ncol + c)),
                  pl.BlockSpec((conv_w.shape[0], tw), lambda b, c, t: (0, c)),
                  pl.BlockSpec((1, tw), lambda b, c, t: (0, c)),
                  pl.BlockSpec((n_heads, hd, 2 * hd), lambda b, c, t: (c, 0, 0)),
                  pl.BlockSpec((1, tw), lambda b, c, t: (0, c)),
                  pl.BlockSpec((1, tw), lambda b, c, t: (0, c)),
                  pl.BlockSpec((1, tw), lambda b, c, t: (0, c))],
        out_specs=pl.BlockSpec((tt, tw), lambda b, c, t: (b * nt + t, c)),
        out_shape=jax.ShapeDtypeStruct((batch * seq, width), BF16),
        scratch_shapes=[pltpu.VMEM((8, tw), F32), pltpu.VMEM((1, tw), F32)],
        compiler_params=_params("parallel", "parallel", "arbitrary"),
        name="rglru",
    )(xrg, xrg, conv_w.astype(F32), row(conv_b), wax, row(ba), row(bx), row(lam))


def kernel(x, p, positions, mix_norm, ffn_norm, ple_norm, final_norm, ab_w_in, ab_w_out, attn_sinks, s5_lam_re, s5_lam_im, s5_log_dt, s5_b_re, s5_b_im, s5_c_re, s5_c_im, s5_d, s5_glu_w, s5_glu_b, cd_w_in, cd_w_out, lru_conv_w, lru_conv_b, lru_wa, lru_ba, lru_wx, lru_bx, lru_lambda, ffn_w_up, ffn_conv_w, ffn_conv_b, ffn_w_down, ple_w_gate, ple_w_proj):
    batch, seq, d_model = x.shape
    depth = mix_norm.shape[0]
    m = batch * seq
    swa_w = (SWA_HEADS + 2 * SWA_KV_HEADS) * HEAD_DIM
    sb_w = 3 * SB_HEADS * HEAD_DIM
    n_steps = (seq // S5_CHUNK - 1).bit_length()
    h = x.reshape(m, d_model).astype(F32)
    for i in range(depth):
        j = i // 2
        n = _rmsnorm(h, mix_norm[i], BF16)
        if i % 2 == 0:
            w_in = ab_w_in[j].astype(BF16)
            qkv = _matmul(n, w_in[:, :swa_w], BF16)
            u = _matmul(n, w_in[:, swa_w:], F32)
            ya = _swa(qkv, positions, attn_sinks[j], batch, seq)
            ops = _s5_operators(s5_lam_re[j], s5_lam_im[j], s5_log_dt[j], s5_b_re[j], s5_b_im[j],
                                s5_c_re[j], s5_c_im[j], s5_d[j], n_steps)
            yb = _glu(_s5(u, ops, batch, seq), s5_glu_w[j].astype(BF16), s5_glu_b[j])
            y = jnp.concatenate([ya, yb], axis=1)
            h = _matmul(y, ab_w_out[j].astype(BF16), F32, residual=h, tn=512)
        else:
            w_in = cd_w_in[j].astype(BF16)
            qkv = _matmul(n, w_in[:, :sb_w], BF16)
            xrg = _matmul(n, w_in[:, sb_w:], F32)
            yc = _stickbreak(qkv, batch, seq)
            yd = _lru(xrg, lru_conv_w[j], lru_conv_b[j], lru_wa[j], lru_ba[j], lru_wx[j],
                      lru_bx[j], lru_lambda[j], batch, seq)
            y = jnp.concatenate([yc, yd], axis=1)
            h = _matmul(y, cd_w_out[j].astype(BF16), F32, residual=h, tn=512)
        n = _rmsnorm(h, ffn_norm[i], BF16)
        act = _ffn_up(n, ffn_w_up[i].astype(BF16), ffn_conv_w[i].astype(F32),
                      ffn_conv_b[i].astype(F32), seq)
        h = _matmul(act, ffn_w_down[i].astype(BF16), F32, residual=h, tn=1024, tk=2048)
        n = _rmsnorm(h, ple_norm[i], BF16)
        h = _ple(n, ple_w_gate[i].astype(BF16), p[i].reshape(m, -1).astype(BF16),
                 ple_w_proj[i].astype(BF16), h)
    return _rmsnorm(h, final_norm, x.dtype).reshape(batch, seq, d_model)
```

```python
import functools
import math

import jax
import jax.numpy as jnp
from jax import lax
from jax.experimental import pallas as pl
from jax.experimental.pallas import tpu as pltpu

F32 = jnp.float32
BF16 = jnp.bfloat16

EPS = 1e-6
HEAD_DIM = 128
SWA_HEADS = 16
SWA_KV_HEADS = 4
SWA_GROUP = SWA_HEADS // SWA_KV_HEADS
SWA_BLOCK = 128
ROPE_THETA = 500000.0
ROPE_DIM = HEAD_DIM // 4
S5_GROUP = 16
S5_STATE = 64
S5_CHUNK = 16
SB_HEADS = 16
LRU_BLOCKS = 16
LRU_C = 8.0
LANES = 128
S5_GROUPS_PER_STEP = LANES // S5_GROUP

VMEM_LIMIT_BYTES = 56 * 1024 * 1024
MXU_COLS = 256
ROW_TILE = 1024


def _params(*semantics):
    return pltpu.CompilerParams(dimension_semantics=semantics,
                                vmem_limit_bytes=VMEM_LIMIT_BYTES)


def _dot(a, b):
    return jnp.dot(a, b, preferred_element_type=F32)


def _dot_nt(a, b):
    return lax.dot_general(a, b, (((1,), (1,)), ((), ())), preferred_element_type=F32)


def _gelu_tanh(x):
    c = math.sqrt(2.0 / math.pi)
    return 0.5 * x * (1.0 + jnp.tanh(c * (x + 0.044715 * (x * x * x))))


def _shift_rows(x, prev8, shift):
    main = pltpu.roll(x, shift, axis=0)
    head = pltpu.roll(jnp.concatenate([prev8, x[:8]], axis=0), shift, axis=0)[8:]
    return jnp.concatenate([head, main[8:]], axis=0)


def _col_chunks(width):
    return [slice(c, c + MXU_COLS) for c in range(0, width, MXU_COLS)]


def _rmsnorm_kernel(x_ref, g_ref, o_ref):
    x = x_ref[...]
    ms = jnp.mean(x * x, axis=-1, keepdims=True)
    o_ref[...] = (x * lax.rsqrt(ms + EPS) * g_ref[...]).astype(o_ref.dtype)


def _rmsnorm(x, g, out_dtype):
    m, d = x.shape
    tm = min(256, m)
    return pl.pallas_call(
        _rmsnorm_kernel,
        grid=(m // tm,),
        in_specs=[pl.BlockSpec((tm, d), lambda i: (i, 0)),
                  pl.BlockSpec((1, d), lambda i: (0, 0))],
        out_specs=pl.BlockSpec((tm, d), lambda i: (i, 0)),
        out_shape=jax.ShapeDtypeStruct((m, d), out_dtype),
        compiler_params=_params("parallel"),
        name="rmsnorm",
    )(x, g.reshape(1, d).astype(F32))


def _proj_kernel(*refs, n_x, has_res):
    x_refs = refs[:n_x]
    w_ref = refs[n_x]
    r_ref = refs[n_x + 1] if has_res else None
    o_ref, wb_ref = refs[n_x + 1 + has_res:]

    @pl.when(pl.program_id(1) == 0)
    def _():
        wb_ref[...] = w_ref[0].astype(BF16)

    for sl in _col_chunks(o_ref.shape[1]):
        acc = r_ref[:, sl] if has_res else None
        k0 = 0
        for x_ref in x_refs:
            kx = x_ref.shape[1]
            part = _dot(x_ref[...], wb_ref[k0:k0 + kx, sl])
            acc = part if acc is None else acc + part
            k0 += kx
        o_ref[:, sl] = acc.astype(o_ref.dtype)


def _proj(xs, w, layer, out_dtype, residual=None, tn=512):
    m = xs[0].shape[0]
    _, kdim, n = w.shape
    tm = min(ROW_TILE, m)
    has_res = residual is not None
    in_specs = [pl.BlockSpec((tm, x.shape[1]), lambda j, i: (i, 0)) for x in xs]
    in_specs.append(pl.BlockSpec((1, kdim, tn), lambda j, i: (layer, 0, j)))
    args = list(xs) + [w]
    if has_res:
        in_specs.append(pl.BlockSpec((tm, tn), lambda j, i: (i, j)))
        args.append(residual)
    return pl.pallas_call(
        functools.partial(_proj_kernel, n_x=len(xs), has_res=has_res),
        grid=(n // tn, m // tm),
        in_specs=in_specs,
        out_specs=pl.BlockSpec((tm, tn), lambda j, i: (i, j)),
        out_shape=jax.ShapeDtypeStruct((m, n), out_dtype),
        scratch_shapes=[pltpu.VMEM((kdim, tn), BF16)],
        compiler_params=_params("arbitrary", "arbitrary"),
        name="proj_res" if has_res else "proj",
    )(*args)


def _ffn_down_kernel(x_ref, w_ref, r_ref, o_ref, acc_ref, *, nk):
    k = pl.program_id(2)
    part = _dot(x_ref[...], w_ref[0])

    @pl.when(k == 0)
    def _():
        acc_ref[...] = part + r_ref[...]

    @pl.when(k > 0)
    def _():
        acc_ref[...] += part

    @pl.when(k == nk - 1)
    def _():
        o_ref[...] = acc_ref[...]


def _ffn_down(x, w, layer, residual, tn=1024, tk=2048):
    m, kdim = x.shape
    n = w.shape[2]
    tm = min(ROW_TILE, m)
    tk = min(tk, kdim)
    nk = kdim // tk
    return pl.pallas_call(
        functools.partial(_ffn_down_kernel, nk=nk),
        grid=(m // tm, n // tn, nk),
        in_specs=[pl.BlockSpec((tm, tk), lambda i, j, k: (i, k)),
                  pl.BlockSpec((1, tk, tn), lambda i, j, k: (layer, k, j)),
                  pl.BlockSpec((tm, tn), lambda i, j, k: (i, j))],
        out_specs=pl.BlockSpec((tm, tn), lambda i, j, k: (i, j)),
        out_shape=jax.ShapeDtypeStruct((m, n), F32),
        scratch_shapes=[pltpu.VMEM((tm, tn), F32)],
        compiler_params=_params("parallel", "parallel", "arbitrary"),
        name="ffn_down_res",
    )(x, w, residual)


def _ffn_up_kernel(x_ref, wg_ref, wu_ref, cwg_ref, cwu_ref, cbg_ref, cbu_ref, o_ref,
                   wb_ref, carry_g, carry_u, *, tiles_per_seq):
    i = pl.program_id(1)
    tm, tn = o_ref.shape

    @pl.when(i == 0)
    def _():
        wb_ref[:, :tn] = wg_ref[0].astype(BF16)
        wb_ref[:, tn:] = wu_ref[0].astype(BF16)

    seq_start = (i % tiles_per_seq) == 0

    def conv(acc, carry_ref, cw_ref, cb_ref):
        prev8 = jnp.where(seq_start, 0.0, carry_ref[...])
        carry_ref[...] = acc[tm - 8:, :]
        x1 = _shift_rows(acc, prev8, 1)
        x2 = _shift_rows(acc, prev8, 2)
        return cb_ref[0] + cw_ref[0, 0:1, :] * x2 + cw_ref[0, 1:2, :] * x1 + cw_ref[0, 2:3, :] * acc

    g = conv(_dot(x_ref[...], wb_ref[:, :tn]), carry_g, cwg_ref, cbg_ref)
    u = conv(_dot(x_ref[...], wb_ref[:, tn:]), carry_u, cwu_ref, cbu_ref)
    o_ref[...] = (g * jax.nn.sigmoid(g) * u).astype(o_ref.dtype)


def _ffn_up(x, w_up, conv_w, conv_b, layer, seq):
    m, d = x.shape
    d_ff = w_up.shape[2] // 2
    tm = min(ROW_TILE, seq)
    tn = MXU_COLS
    nj = d_ff // tn
    taps = conv_w.shape[1]
    return pl.pallas_call(
        functools.partial(_ffn_up_kernel, tiles_per_seq=seq // tm),
        grid=(nj, m // tm),
        in_specs=[pl.BlockSpec((tm, d), lambda j, i: (i, 0)),
                  pl.BlockSpec((1, d, tn), lambda j, i: (layer, 0, j)),
                  pl.BlockSpec((1, d, tn), lambda j, i: (layer, 0, j + nj)),
                  pl.BlockSpec((1, taps, tn), lambda j, i: (layer, 0, j)),
                  pl.BlockSpec((1, taps, tn), lambda j, i: (layer, 0, j + nj)),
                  pl.BlockSpec((1, 1, tn), lambda j, i: (layer, 0, j)),
                  pl.BlockSpec((1, 1, tn), lambda j, i: (layer, 0, j + nj))],
        out_specs=pl.BlockSpec((tm, tn), lambda j, i: (i, j)),
        out_shape=jax.ShapeDtypeStruct((m, d_ff), BF16),
        scratch_shapes=[pltpu.VMEM((d, 2 * tn), BF16),
                        pltpu.VMEM((8, tn), F32), pltpu.VMEM((8, tn), F32)],
        compiler_params=_params("arbitrary", "arbitrary"),
        name="ffn_up_conv_gate",
    )(x, w_up, w_up, conv_w, conv_w, conv_b, conv_b)


def _ple_kernel(n_ref, wg_ref, p_ref, wp_ref, h_ref, o_ref, wb_ref):
    @pl.when(pl.program_id(1) == 0)
    def _():
        wb_ref[...] = wg_ref[0].astype(BF16)

    pb = p_ref[0].astype(BF16)
    for sl in _col_chunks(o_ref.shape[1]):
        gate = jax.nn.sigmoid(_dot(n_ref[...], wb_ref[:, sl]))
        proj = _dot(pb, wp_ref[0, :, sl].astype(BF16))
        o_ref[:, sl] = h_ref[:, sl] + gate * proj


def _ple(n, wg, p, wp, layer, h, tn=512):
    m, d = n.shape
    pd = p.shape[2]
    tm = min(ROW_TILE, m)
    return pl.pallas_call(
        _ple_kernel,
        grid=(d // tn, m // tm),
        in_specs=[pl.BlockSpec((tm, d), lambda j, i: (i, 0)),
                  pl.BlockSpec((1, d, tn), lambda j, i: (layer, 0, j)),
                  pl.BlockSpec((1, tm, pd), lambda j, i: (layer, i, 0)),
                  pl.BlockSpec((1, pd, tn), lambda j, i: (layer, 0, j)),
                  pl.BlockSpec((tm, tn), lambda j, i: (i, j))],
        out_specs=pl.BlockSpec((tm, tn), lambda j, i: (i, j)),
        out_shape=jax.ShapeDtypeStruct((m, d), F32),
        scratch_shapes=[pltpu.VMEM((d, tn), BF16)],
        compiler_params=_params("arbitrary", "arbitrary"),
        name="ple_gate",
    )(n, wg, p, wp, h)


def _swa_kernel(sink_ref, inv_ref, q_ref, kc_ref, kp_ref, vc_ref, vp_ref, pc_ref, pp_ref,
                o_ref):
    n = pl.program_id(1)
    blk = SWA_BLOCK
    half = ROPE_DIM // 2
    lane = lax.broadcasted_iota(jnp.int32, (1, HEAD_DIM), 1)

    def tables(pos_ref):
        ang = pos_ref[...].astype(F32) * inv_ref[...]
        cos, sin = jnp.cos(ang), jnp.sin(ang)
        from_below = jnp.where((lane >= half) & (lane < 2 * half), sin, 0.0)
        from_above = jnp.where(lane < half, -sin, 0.0)
        return cos, from_below, from_above

    def rope(x, tab):
        cos, from_below, from_above = tab
        return (x * cos + pltpu.roll(x, half, axis=1) * from_below
                + pltpu.roll(x, HEAD_DIM - half, axis=1) * from_above).astype(BF16)

    tab_c = tables(pc_ref)
    tab_p = tables(pp_ref)
    row = lax.broadcasted_iota(jnp.int32, (SWA_GROUP * blk, 2 * blk), 0) % blk
    col = lax.broadcasted_iota(jnp.int32, (SWA_GROUP * blk, 2 * blk), 1)
    valid = (col > row) & (col <= row + blk) & ((col >= blk) | (n > 0))
    scale = HEAD_DIM ** -0.5

    for kh in range(SWA_KV_HEADS):
        ksl = slice(kh * HEAD_DIM, (kh + 1) * HEAD_DIM)
        k2 = jnp.concatenate([rope(kp_ref[:, ksl], tab_p), rope(kc_ref[:, ksl], tab_c)], axis=0)
        v2 = jnp.concatenate([vp_ref[:, ksl], vc_ref[:, ksl]], axis=0).astype(BF16)
        heads = [kh * SWA_GROUP + g for g in range(SWA_GROUP)]
        qs = jnp.concatenate(
            [rope(q_ref[:, h * HEAD_DIM:(h + 1) * HEAD_DIM], tab_c) for h in heads], axis=0)
        sink = jnp.concatenate(
            [jnp.full((blk, 1), sink_ref[h], F32) for h in heads], axis=0)
        logits = jnp.where(valid, _dot_nt(qs, k2) * scale, -jnp.inf)
        mx = jnp.maximum(jnp.max(logits, axis=-1, keepdims=True), sink)
        w = jnp.exp(logits - mx)
        den = jnp.sum(w, axis=-1, keepdims=True) + jnp.exp(sink - mx)
        out = _dot((w / den).astype(BF16), v2)
        for g, h in enumerate(heads):
            o_ref[:, h * HEAD_DIM:(h + 1) * HEAD_DIM] = out[g * blk:(g + 1) * blk].astype(o_ref.dtype)


def _swa(proj, positions, sinks, batch, seq):
    blk = SWA_BLOCK
    nb = seq // blk
    qw = SWA_HEADS * HEAD_DIM
    kw = SWA_KV_HEADS * HEAD_DIM
    kcol = qw // kw
    half = ROPE_DIM // 2
    inv = ROPE_THETA ** (-jnp.arange(half, dtype=F32) * (2.0 / ROPE_DIM))
    inv = jnp.concatenate([inv, inv, jnp.zeros((HEAD_DIM - 2 * half,), F32)]).reshape(1, HEAD_DIM)
    pos = positions.reshape(batch * seq, 1)

    def cur(b, n):
        return b * nb + n

    def prev(b, n):
        return b * nb + jnp.maximum(n - 1, 0)

    return pl.pallas_call(
        _swa_kernel,
        grid=(batch, nb),
        in_specs=[pl.BlockSpec(memory_space=pltpu.SMEM),
                  pl.BlockSpec((1, HEAD_DIM), lambda b, n: (0, 0)),
                  pl.BlockSpec((blk, qw), lambda b, n: (cur(b, n), 0)),
                  pl.BlockSpec((blk, kw), lambda b, n: (cur(b, n), kcol)),
                  pl.BlockSpec((blk, kw), lambda b, n: (prev(b, n), kcol)),
                  pl.BlockSpec((blk, kw), lambda b, n: (cur(b, n), kcol + 1)),
                  pl.BlockSpec((blk, kw), lambda b, n: (prev(b, n), kcol + 1)),
                  pl.BlockSpec((blk, 1), lambda b, n: (cur(b, n), 0)),
                  pl.BlockSpec((blk, 1), lambda b, n: (prev(b, n), 0))],
        out_specs=pl.BlockSpec((blk, qw), lambda b, n: (cur(b, n), 0)),
        out_shape=jax.ShapeDtypeStruct((batch * seq, qw), BF16),
        compiler_params=_params("parallel", "parallel"),
        name="swa_attention",
    )(sinks.astype(F32), inv, proj, proj, proj, proj, proj, pos, pos)


def _s5_operators(lam_re, lam_im, log_dt, b_re, b_im, c_re, c_im, d_skip, n_steps):
    hi = lax.Precision.HIGHEST
    t = S5_CHUNK
    g, n = lam_re.shape
    p = b_re.shape[-1]
    gs = S5_GROUPS_PER_STEP
    nb = g // gs
    dt = jnp.exp(log_dt.astype(F32))[:, None]
    lr = jnp.minimum(lam_re.astype(F32), -1e-4)
    li = lam_im.astype(F32)
    mag = jnp.exp(lr * dt)
    ab_re, ab_im = mag * jnp.cos(li * dt), mag * jnp.sin(li * dt)
    nr, ni = ab_re - 1.0, ab_im
    den = lr * lr + li * li
    f_re, f_im = (nr * lr + ni * li) / den, (ni * lr - nr * li) / den
    br, bi = b_re.astype(F32), b_im.astype(F32)
    bb_re = f_re[..., None] * br - f_im[..., None] * bi
    bb_im = f_re[..., None] * bi + f_im[..., None] * br

    def power(k):
        k = k.astype(F32)[None, :, None]
        m = jnp.exp(k * (lr * dt)[:, None, :])
        ph = k * (li * dt)[:, None, :]
        return m * jnp.cos(ph), m * jnp.sin(ph)

    pw_re, pw_im = power(jnp.arange(t + 1))
    cr, ci = c_re.astype(F32), c_im.astype(F32)
    ca_re = cr[:, None] * pw_re[:, :, None, :] - ci[:, None] * pw_im[:, :, None, :]
    ca_im = cr[:, None] * pw_im[:, :, None, :] + ci[:, None] * pw_re[:, :, None, :]
    kern = (jnp.einsum('gkpn,gnq->gkpq', ca_re[:, :t], bb_re, precision=hi)
            - jnp.einsum('gkpn,gnq->gkpq', ca_im[:, :t], bb_im, precision=hi))
    rev_re, rev_im = pw_re[:, t - 1::-1], pw_im[:, t - 1::-1]
    st_re = rev_re[..., None] * bb_re[:, None] - rev_im[..., None] * bb_im[:, None]
    st_im = rev_re[..., None] * bb_im[:, None] + rev_im[..., None] * bb_re[:, None]

    eye = jnp.eye(gs, dtype=F32)

    def blockdiag(a, spec):
        return jnp.einsum(spec, a.reshape((nb, gs) + a.shape[1:]), eye)

    local_op = blockdiag(kern, 'bgkpq,gh->bkgqhp').reshape(nb, t, gs * p, gs * p)
    state_op = jnp.concatenate(
        [blockdiag(st_re, 'bgjnq,gh->bjgqhn').reshape(nb, t, gs * p, gs * n),
         blockdiag(st_im, 'bgjnq,gh->bjgqhn').reshape(nb, t, gs * p, gs * n)], axis=-1)
    carry_op = jnp.concatenate(
        [blockdiag(ca_re[:, 1:], 'bgipn,gh->bignhp').reshape(nb, t, gs * n, gs * p),
         blockdiag(-ca_im[:, 1:], 'bgipn,gh->bignhp').reshape(nb, t, gs * n, gs * p)], axis=2)

    ar, ai = power(t * (2 ** jnp.arange(n_steps)))
    flat = lambda a: a.reshape(nb, gs, n_steps, n).transpose(0, 2, 1, 3).reshape(nb, n_steps, gs * n)
    ar, ai = flat(ar), flat(ai)
    step_mul = jnp.stack([jnp.concatenate([ar, ar], axis=-1),
                          jnp.concatenate([-ai, ai], axis=-1)], axis=2)
    step_mul = step_mul.reshape(nb, 2 * n_steps, 2 * gs * n)
    d_rep = d_skip.astype(F32).reshape(nb, 1, gs * p)
    return (local_op.astype(BF16), state_op.astype(BF16), carry_op.astype(BF16), step_mul, d_rep)


def _s5_kernel(u_ref, local_ref, state_ref, carry_ref, mul_ref, d_ref, o_ref, xs_ref, ys_ref, *,
               chunks_per_seq, n_steps):
    nc, t, _ = u_ref.shape
    rows = lambda i, cnt=1: slice(i * nc, (i + cnt) * nc)
    for j in range(t):
        xs_ref[rows(j), :] = u_ref[:, j, :].astype(BF16)
    h = _dot(xs_ref[rows(0), :], state_ref[0, 0])
    for j in range(1, t):
        h = h + _dot(xs_ref[rows(j), :], state_ref[0, j])
    half = h.shape[1] // 2
    row = lax.broadcasted_iota(jnp.int32, h.shape, 0) % chunks_per_seq
    for s in range(n_steps):
        sh = 1 << s
        hs = jnp.where(row >= sh, pltpu.roll(h, sh, axis=0), 0.0)
        h = (h + mul_ref[0, 2 * s:2 * s + 1, :] * hs
             + mul_ref[0, 2 * s + 1:2 * s + 2, :] * pltpu.roll(hs, half, axis=1))
    h_prev = jnp.where(row >= 1, pltpu.roll(h, 1, axis=0), 0.0).astype(BF16)
    ys_ref[...] = _dot(xs_ref[...], local_ref[0, 0])
    for lag in range(1, t):
        ys_ref[rows(lag, t - lag), :] += _dot(xs_ref[rows(0, t - lag), :], local_ref[0, lag])
    for i in range(t):
        y = ys_ref[rows(i), :] + _dot(h_prev, carry_ref[0, i]) + d_ref[0] * u_ref[:, i, :]
        o_ref[:, i, :] = _gelu_tanh(y)


def _s5(proj, col0, ops, batch, seq):
    local_op, state_op, carry_op, step_mul, d_rep = ops
    nb, t, lanes, _ = local_op.shape
    nc = batch * seq // t
    ns = state_op.shape[3]
    n_steps = step_mul.shape[1] // 2
    c0 = col0 // lanes
    out = pl.pallas_call(
        functools.partial(_s5_kernel, chunks_per_seq=seq // t, n_steps=n_steps),
        grid=(nb,),
        in_specs=[pl.BlockSpec((nc, t, lanes), lambda b: (0, 0, c0 + b)),
                  pl.BlockSpec((1, t, lanes, lanes), lambda b: (b, 0, 0, 0)),
                  pl.BlockSpec((1, t, lanes, ns), lambda b: (b, 0, 0, 0)),
                  pl.BlockSpec((1, t, ns, lanes), lambda b: (b, 0, 0, 0)),
                  pl.BlockSpec((1, 2 * n_steps, ns), lambda b: (b, 0, 0)),
                  pl.BlockSpec((1, 1, lanes), lambda b: (b, 0, 0))],
        out_specs=pl.BlockSpec((nc, t, lanes), lambda b: (0, 0, b)),
        out_shape=jax.ShapeDtypeStruct((nc, t, nb * lanes), F32),
        scratch_shapes=[pltpu.VMEM((t * nc, lanes), BF16), pltpu.VMEM((t * nc, lanes), F32)],
        compiler_params=_params("parallel"),
        name="s5_chunked",
    )(proj.reshape(nc, t, proj.shape[1]), local_op, state_op, carry_op, step_mul, d_rep)
    return out.reshape(batch * seq, nb * lanes)


def _glu_kernel(g_ref, w_ref, b_ref, o_ref):
    g = g_ref[...]
    z = _dot(g.astype(BF16), w_ref[...]) + b_ref[...]
    o_ref[...] = (g * jax.nn.sigmoid(z)).astype(o_ref.dtype)


def _glu(g, w, b):
    m, d = g.shape
    tm = min(512, m)
    return pl.pallas_call(
        _glu_kernel,
        grid=(m // tm,),
        in_specs=[pl.BlockSpec((tm, d), lambda i: (i, 0)),
                  pl.BlockSpec((d, d), lambda i: (0, 0)),
                  pl.BlockSpec((1, d), lambda i: (0, 0))],
        out_specs=pl.BlockSpec((tm, d), lambda i: (i, 0)),
        out_shape=jax.ShapeDtypeStruct((m, d), BF16),
        compiler_params=_params("parallel"),
        name="s5_glu",
    )(g, w, b.reshape(1, d).astype(F32))


def _stickbreak_kernel(q_ref, k_ref, v_ref, o_ref):
    qi = pl.program_id(2)
    tq = q_ref.shape[0]
    blk = LANES
    nblk = tq // blk
    scale = HEAD_DIM ** -0.5
    q = q_ref[...].astype(BF16)
    suffix = (lax.broadcasted_iota(jnp.int32, (blk, blk), 0)
              > lax.broadcasted_iota(jnp.int32, (blk, blk), 1)).astype(BF16)

    def group(start, acc, later, on_diagonal):
        k = k_ref[pl.ds(start, tq), :].astype(BF16)
        v = v_ref[pl.ds(start, tq), :].astype(BF16)
        z = _dot_nt(q, k) * scale
        softplus = jnp.log1p(jnp.exp(-jnp.abs(z)))
        log_beta = jnp.minimum(z, 0.0) - softplus
        log_stay = log_beta - z
        if on_diagonal:
            strict = (lax.broadcasted_iota(jnp.int32, (tq, tq), 1)
                      < lax.broadcasted_iota(jnp.int32, (tq, tq), 0))
            log_stay = jnp.where(strict, log_stay, 0.0)
        blocks = [log_stay[:, b * blk:(b + 1) * blk] for b in range(nblk)]
        stacked = jnp.concatenate(blocks, axis=0)
        hi = stacked.astype(BF16)
        lo = (stacked - hi.astype(F32)).astype(BF16)
        cum = _dot(hi, suffix) + _dot(lo, suffix)
        pieces = [None] * nblk
        for b in reversed(range(nblk)):
            pieces[b] = cum[b * tq:(b + 1) * tq] + later
            later = later + jnp.sum(blocks[b], axis=-1, keepdims=True)
        w = jnp.exp(log_beta + jnp.concatenate(pieces, axis=1))
        if on_diagonal:
            w = jnp.where(strict, w, 0.0)
        return acc + _dot(w.astype(BF16), v), later

    q0 = pl.multiple_of(qi * tq, tq)
    acc, later = group(q0, jnp.zeros((tq, HEAD_DIM), F32), jnp.zeros((tq, 1), F32), True)

    def body(it, carry):
        start = pl.multiple_of(q0 - (it + 1) * tq, tq)
        return group(start, carry[0], carry[1], False)

    acc, later = lax.fori_loop(0, qi, body, (acc, later))
    o_ref[...] = acc.astype(o_ref.dtype)


def _stickbreak(proj, batch, seq):
    tq = min(512, seq)
    nq = seq // tq
    hd = HEAD_DIM
    return pl.pallas_call(
        _stickbreak_kernel,
        grid=(batch, SB_HEADS, nq),
        in_specs=[pl.BlockSpec((tq, hd), lambda b, h, i: (b * nq + i, h)),
                  pl.BlockSpec((seq, hd), lambda b, h, i: (b, SB_HEADS + h)),
                  pl.BlockSpec((seq, hd), lambda b, h, i: (b, 2 * SB_HEADS + h))],
        out_specs=pl.BlockSpec((tq, hd), lambda b, h, i: (b * nq + i, h)),
        out_shape=jax.ShapeDtypeStruct((batch * seq, SB_HEADS * hd), BF16),
        compiler_params=_params("parallel", "parallel", "arbitrary"),
        name="stickbreak_attention",
    )(proj, proj, proj)


def _lru_kernel(xr_ref, xg_ref, cw_ref, cb_ref, wax_ref, ba_ref, bx_ref, lam_ref, o_ref,
                x_tail, h_last, *, n_heads):
    t = pl.program_id(2)
    tt, width = o_ref.shape
    hd = width // n_heads
    x = xr_ref[...]
    prev8 = jnp.where(t == 0, 0.0, x_tail[...])
    x_tail[...] = x[tt - 8:, :]
    xc = (cb_ref[...] + cw_ref[0:1, :] * _shift_rows(x, prev8, 3)
          + cw_ref[1:2, :] * _shift_rows(x, prev8, 2)
          + cw_ref[2:3, :] * _shift_rows(x, prev8, 1) + cw_ref[3:4, :] * x)
    gates = [_dot(xc[:, h * hd:(h + 1) * hd].astype(BF16), wax_ref[h]) for h in range(n_heads)]
    r = jax.nn.sigmoid(jnp.concatenate([gz[:, :hd] for gz in gates], axis=1) + ba_ref[...])
    ig = jax.nn.sigmoid(jnp.concatenate([gz[:, hd:] for gz in gates], axis=1) + bx_ref[...])
    lam = lam_ref[...]
    softplus_neg_lam = jnp.maximum(-lam, 0.0) + jnp.log1p(jnp.exp(-jnp.abs(lam)))
    log_a = -LRU_C * r * softplus_neg_lam
    a = jnp.exp(log_a)
    drive = jnp.sqrt(-jnp.tanh(log_a) * (1.0 + a * a)) * (ig * xc)
    row = lax.broadcasted_iota(jnp.int32, (tt, width), 0)
    h = drive
    s = 1
    while s < tt:
        keep = row >= s
        h = a * jnp.where(keep, pltpu.roll(h, s, axis=0), 0.0) + h
        a = a * jnp.where(keep, pltpu.roll(a, s, axis=0), 1.0)
        s *= 2
    h0 = jnp.where(t == 0, 0.0, h_last[...])
    h = h + a * h0
    h_last[...] = h[tt - 1:, :]
    o_ref[...] = (h * _gelu_tanh(xg_ref[...])).astype(o_ref.dtype)


def _lru(proj, col0, conv_w, conv_b, wa, ba, wx, bx, lam, batch, seq):
    width = wa.shape[0] * wa.shape[1]
    hd = width // LRU_BLOCKS
    n_heads = 4
    tw = n_heads * hd
    ncol = width // tw
    c0 = col0 // tw
    tt = min(256, seq)
    nt = seq // tt
    wax = jnp.concatenate([wa, wx], axis=-1).astype(BF16)
    row = lambda v: v.reshape(1, width).astype(F32)
    return pl.pallas_call(
        functools.partial(_lru_kernel, n_heads=n_heads),
        grid=(batch, ncol, nt),
        in_specs=[pl.BlockSpec((tt, tw), lambda b, c, t: (b * nt + t, c0 + c)),
                  pl.BlockSpec((tt, tw), lambda b, c, t: (b * nt + t, c0 + ncol + c)),
                  pl.BlockSpec((conv_w.shape[0], tw), lambda b, c, t: (0, c)),
                  pl.BlockSpec((1, tw), lambda b, c, t: (0, c)),
                  pl.BlockSpec((n_heads, hd, 2 * hd), lambda b, c, t: (c, 0, 0)),
                  pl.BlockSpec((1, tw), lambda b, c, t: (0, c)),
                  pl.BlockSpec((1, tw), lambda b, c, t: (0, c)),
                  pl.BlockSpec((1, tw), lambda b, c, t: (0, c))],
        out_specs=pl.BlockSpec((tt, tw), lambda b, c, t: (b * nt + t, c)),
        out_shape=jax.ShapeDtypeStruct((batch * seq, width), BF16),
        scratch_shapes=[pltpu.VMEM((8, tw), F32), pltpu.VMEM((1, tw), F32)],
        compiler_params=_params("parallel", "parallel", "arbitrary"),
        name="rglru",
    )(proj, proj, conv_w.astype(F32), row(conv_b), wax, row(ba), row(bx), row(lam))


def kernel(x, p, positions, mix_norm, ffn_norm, ple_norm, final_norm, ab_w_in, ab_w_out, attn_sinks, s5_lam_re, s5_lam_im, s5_log_dt, s5_b_re, s5_b_im, s5_c_re, s5_c_im, s5_d, s5_glu_w, s5_glu_b, cd_w_in, cd_w_out, lru_conv_w, lru_conv_b, lru_wa, lru_ba, lru_wx, lru_bx, lru_lambda, ffn_w_up, ffn_conv_w, ffn_conv_b, ffn_w_down, ple_w_gate, ple_w_proj):
    batch, seq, d_model = x.shape
    depth = mix_norm.shape[0]
    m = batch * seq
    swa_w = (SWA_HEADS + 2 * SWA_KV_HEADS) * HEAD_DIM
    sb_w = 3 * SB_HEADS * HEAD_DIM
    n_steps = (seq // S5_CHUNK - 1).bit_length()
    w_down = ffn_w_down.astype(BF16)
    glu_w = s5_glu_w.astype(BF16)
    conv_b = ffn_conv_b.reshape(depth, 1, -1).astype(F32)
    conv_w = ffn_conv_w.astype(F32)
    p_rows = p.reshape(depth, m, -1)
    h = x.reshape(m, d_model).astype(F32)
    for i in range(depth):
        j = i // 2
        n = _rmsnorm(h, mix_norm[i], BF16)
        if i % 2 == 0:
            proj = _proj([n], ab_w_in, j, F32)
            ya = _swa(proj, positions, attn_sinks[j], batch, seq)
            ops = _s5_operators(s5_lam_re[j], s5_lam_im[j], s5_log_dt[j], s5_b_re[j], s5_b_im[j],
                                s5_c_re[j], s5_c_im[j], s5_d[j], n_steps)
            yb = _glu(_s5(proj, swa_w, ops, batch, seq), glu_w[j], s5_glu_b[j])
            h = _proj([ya, yb], ab_w_out, j, F32, residual=h)
        else:
            proj = _proj([n], cd_w_in, j, F32)
            yc = _stickbreak(proj, batch, seq)
            yd = _lru(proj, sb_w, lru_conv_w[j], lru_conv_b[j], lru_wa[j], lru_ba[j], lru_wx[j],
                      lru_bx[j], lru_lambda[j], batch, seq)
            h = _proj([yc, yd], cd_w_out, j, F32, residual=h)
        n = _rmsnorm(h, ffn_norm[i], BF16)
        act = _ffn_up(n, ffn_w_up, conv_w, conv_b, i, seq)
        h = _ffn_down(act, w_down, i, h)
        n = _rmsnorm(h, ple_norm[i], BF16)
        h = _ple(n, ple_w_gate, p_rows, ple_w_proj, i, h)
    return _rmsnorm(h, final_norm, x.dtype).reshape(batch, seq, d_model)
```
